```python
import math
import jax, jax.numpy as jnp
from jax import lax
import numpy as np

D_MODEL = 1024
BATCH = 32
SEQ = 256
DEPTH = 4
DEC_BATCH = 2
DEC_SEQ = 1024
PAST_LEN = 512

GRID_W = 64
N_ATTN_LAYERS = (DEPTH + 1) // 2
N_HYENA_LAYERS = DEPTH // 2
H_A = 8
KV_A = 2
HD_A = 64
WINDOW = 128
Q_BLOCK = 128
H_B = 8
NOPE_B = 64
ROPE_B = 32
QK_B = NOPE_B + ROPE_B
V_B = 64
Q_LORA = 384
KV_LORA = 256
AB_IN = H_A * HD_A + 2 * KV_A * HD_A + Q_LORA + KV_LORA + ROPE_B
AB_OUT = H_A * HD_A + H_B * V_B
ROPE_THETA = 10000.0
HYENA_ORDER = 2
N_BANDS = 16
FILTER_EMB = 1 + 2 * N_BANDS
FILTER_HID = 64
FAST_DECAY_PCT = 0.3
SLOW_DECAY_PCT = 1.5
DECAY_TARGET = 1e-2
N_EXPERTS = 16
N_GROUPS = 4
EXPERTS_PER_GROUP = N_EXPERTS // N_GROUPS
TOP_K = 2
D_FF = 512
EPS = 1e-6
NEG_INF = -1e30

kernel_name = "hybrid_diffusion_prefix_step"


def rms_norm(x, g):
    xf = x.astype(jnp.float32)
    y = xf * lax.rsqrt(jnp.mean(xf * xf, axis=-1, keepdims=True) + EPS)
    return (y * g.astype(jnp.float32)).astype(x.dtype)


def adaln(cond, w, b):
    m = jax.nn.silu(cond) @ w + b
    return m.reshape(cond.shape[:-1] + (1, 6, D_MODEL))


def axial_rope(x):
    L, d = x.shape[-2], x.shape[-1]
    rows = L // GRID_W
    row = jnp.repeat(jnp.arange(rows, dtype=jnp.float32), GRID_W)
    col = jnp.tile(jnp.arange(GRID_W, dtype=jnp.float32), rows)
    half, quarter = d // 2, d // 4
    freqs = ROPE_THETA ** (-jnp.arange(quarter, dtype=jnp.float32) / quarter)

    def rot(xh, pos):
        ang = pos[:, None] * freqs[None, :]
        cos, sin = jnp.cos(ang).astype(x.dtype), jnp.sin(ang).astype(x.dtype)
        x1, x2 = xh[..., :quarter], xh[..., quarter:]
        return jnp.concatenate([x1 * cos - x2 * sin, x1 * sin + x2 * cos], axis=-1)

    return jnp.concatenate([rot(x[..., :half], row), rot(x[..., half:], col)], axis=-1)


def rope_tail(x):
    return jnp.concatenate([x[..., :NOPE_B], axial_rope(x[..., NOPE_B:])], axis=-1)


def attn_sweep(q, k, v, sink):
    B, H, Lq, d = q.shape
    Hk, dv = k.shape[1], v.shape[-1]
    G, nb = H // Hk, Lq // Q_BLOCK
    qb = q.reshape(B, Hk, G, nb, Q_BLOCK, d).transpose(3, 0, 1, 2, 4, 5)
    scale = d ** -0.5

    def one_block(q_blk):
        s = jnp.einsum('bkgqd,bksd->bkgqs', q_blk, k).astype(jnp.float32) * scale
        if sink is not None:
            sk = jnp.broadcast_to(sink.astype(jnp.float32).reshape(1, Hk, G, 1, 1), s.shape[:-1] + (1,))
            s = jnp.concatenate([s, sk], axis=-1)
        p = jax.nn.softmax(s, axis=-1)
        if sink is not None:
            p = p[..., :-1]
        return jnp.einsum('bkgqs,bksd->bkgqd', p.astype(v.dtype), v)

    o = lax.map(one_block, qb)
    return o.transpose(1, 2, 3, 0, 4, 5).reshape(B, H, Lq, dv)


def band_attn(q, k, v, k_ctx, v_ctx, sink):
    B, H, L, d = q.shape
    Hk = k.shape[1]
    G, nb = H // Hk, L // Q_BLOCK
    C = k_ctx.shape[2]
    scale = d ** -0.5
    qb = q.reshape(B, Hk, G, nb, Q_BLOCK, d)

    def bands(x):
        xp = jnp.pad(x, ((0, 0), (0, 0), (Q_BLOCK, Q_BLOCK), (0, 0)))
        xp = xp.reshape(B, Hk, nb + 2, Q_BLOCK, x.shape[-1])
        return jnp.concatenate([xp[:, :, :-2], xp[:, :, 1:-1], xp[:, :, 2:]], axis=3)

    kb, vb = bands(k), bands(v)
    s_band = jnp.einsum('bkgnqd,bknsd->bkgnqs', qb, kb).astype(jnp.float32) * scale
    a = jnp.arange(Q_BLOCK)[:, None]
    bi = jnp.arange(3 * Q_BLOCK)[None, :]
    blk = jnp.arange(nb)[:, None, None]
    pos = blk * Q_BLOCK - Q_BLOCK + bi
    rel = Q_BLOCK + a - bi
    mask = (jnp.abs(rel) <= WINDOW) & (pos >= 0) & (pos < L)
    s_band = jnp.where(mask, s_band, NEG_INF)
    s_ctx = jnp.einsum('bkgnqd,bkcd->bkgnqc', qb, k_ctx).astype(jnp.float32) * scale
    sk = jnp.broadcast_to(sink.astype(jnp.float32).reshape(1, Hk, G, 1, 1, 1), s_band.shape[:-1] + (1,))
    p = jax.nn.softmax(jnp.concatenate([s_band, s_ctx, sk], axis=-1), axis=-1).astype(v.dtype)
    nband = 3 * Q_BLOCK
    o = (jnp.einsum('bkgnqs,bknsd->bkgnqd', p[..., :nband], vb)
         + jnp.einsum('bkgnqc,bkcd->bkgnqd', p[..., nband:nband + C], v_ctx))
    return o.reshape(B, H, L, d)


def ab_project(h, w_in, a_qn, a_kn, b_qan, b_wuq, b_kvan, b_qn):
    B, L, _ = h.shape
    p = h @ w_in
    o1 = H_A * HD_A
    o2 = o1 + KV_A * HD_A
    o3 = o2 + KV_A * HD_A
    o4 = o3 + Q_LORA
    o5 = o4 + KV_LORA
    o6 = o5 + ROPE_B
    q_a = rms_norm(p[..., :o1].reshape(B, L, H_A, HD_A).transpose(0, 2, 1, 3), a_qn)
    k_a = rms_norm(p[..., o1:o2].reshape(B, L, KV_A, HD_A).transpose(0, 2, 1, 3), a_kn)
    v_a = p[..., o2:o3].reshape(B, L, KV_A, HD_A).transpose(0, 2, 1, 3)
    q_b = rms_norm(p[..., o3:o4], b_qan) @ b_wuq
    q_b = rms_norm(q_b.reshape(B, L, H_B, QK_B).transpose(0, 2, 1, 3), b_qn)
    ckv = rms_norm(p[..., o4:o5], b_kvan)
    kpe = p[..., o5:o6]
    return q_a, k_a, v_a, q_b, ckv, kpe


def mla_kv(ckv, kpe, w_ukv, k_norm):
    B, L, _ = ckv.shape
    kv = (ckv @ w_ukv).reshape(B, L, H_B, NOPE_B + V_B).transpose(0, 2, 1, 3)
    k_nope, v = kv[..., :NOPE_B], kv[..., NOPE_B:]
    k = jnp.concatenate([k_nope, jnp.broadcast_to(kpe[:, None], (B, H_B, L, ROPE_B))], axis=-1)
    return rms_norm(k, k_norm), v


def ab_merge(out_a, out_b, w_out):
    B, _, L, _ = out_a.shape
    oa = out_a.transpose(0, 2, 1, 3).reshape(B, L, H_A * HD_A)
    ob = out_b.transpose(0, 2, 1, 3).reshape(B, L, H_B * V_B)
    return jnp.concatenate([oa, ob], axis=-1) @ w_out


def ab_context(h, w_in, a_qn, a_kn, a_sink, b_qan, b_wuq, b_kvan, b_wukv, b_qn, b_kn, w_out):
    q_a, k_a, v_a, q_b, ckv, kpe = ab_project(h, w_in, a_qn, a_kn, b_qan, b_wuq, b_kvan, b_qn)
    out_a = attn_sweep(q_a, k_a, v_a, a_sink)
    k_b, v_b = mla_kv(ckv, kpe, b_wukv, b_kn)
    out_b = attn_sweep(q_b, k_b, v_b, None)
    return ab_merge(out_a, out_b, w_out), k_a, v_a, ckv, kpe


def ab_latent(h, ctx_k, ctx_v, ctx_ckv, ctx_kpe, w_in, a_qn, a_kn, a_sink, b_qan, b_wuq, b_kvan,
              b_wukv, b_qn, b_kn, w_out):
    q_a, k_a, v_a, q_b, ckv, kpe = ab_project(h, w_in, a_qn, a_kn, b_qan, b_wuq, b_kvan, b_qn)
    out_a = band_attn(axial_rope(q_a), axial_rope(k_a), v_a, ctx_k, ctx_v, a_sink)
    k_bl, v_bl = mla_kv(ckv, kpe, b_wukv, b_kn)
    k_bc, v_bc = mla_kv(ctx_ckv, ctx_kpe, b_wukv, b_kn)
    k_b = jnp.concatenate([rope_tail(k_bl), k_bc], axis=2)
    v_b = jnp.concatenate([v_bl, v_bc], axis=2)
    out_b = attn_sweep(rope_tail(q_b), k_b, v_b, None)
    return ab_merge(out_a, out_b, w_out)


def implicit_filter_spectra(L, f_w1, f_b1, f_freq, f_w2, f_b2, f_w3):
    f32 = jnp.float32
    t = jnp.arange(L, dtype=f32) / L
    bands = jnp.arange(1, N_BANDS + 1, dtype=f32)
    ang = 2.0 * math.pi * t[:, None] * bands[None, :]
    feats = jnp.concatenate([t[:, None], jnp.cos(ang), jnp.sin(ang)], axis=-1)
    hdn = jnp.sin(f_freq[0].astype(f32) * (feats @ f_w1.astype(f32) + f_b1.astype(f32)))
    hdn = jnp.sin(f_freq[1].astype(f32) * (hdn @ f_w2.astype(f32) + f_b2.astype(f32)))
    taps = (hdn @ f_w3.astype(f32)).reshape(L, HYENA_ORDER, 2, D_MODEL)
    max_decay = math.log(DECAY_TARGET) / FAST_DECAY_PCT
    min_decay = math.log(DECAY_TARGET) / SLOW_DECAY_PCT
    deltas = jnp.abs(jnp.linspace(min_decay, max_decay, D_MODEL, dtype=f32))
    taps = taps * jnp.exp(-t[:, None] * deltas[None, :])[:, None, None, :]
    fwd, bwd = taps[:, :, 0], taps[:, :, 1]
    k2 = jnp.concatenate([fwd, jnp.zeros((1, HYENA_ORDER, D_MODEL), f32), bwd[1:][::-1]], axis=0)
    return jnp.fft.rfft(k2, axis=0)


def fft_long_conv(z, kf, bias):
    L = z.shape[1]
    zf32 = z.astype(jnp.float32)
    zf = jnp.fft.rfft(zf32, n=2 * L, axis=1)
    y = jnp.fft.irfft(zf * kf[None], n=2 * L, axis=1)[:, :L]
    return (y + zf32 * bias.astype(jnp.float32)).astype(z.dtype)


def hyena(h, w_in, conv_w, conv_b, f_w1, f_b1, f_freq, f_w2, f_b2, f_w3, f_bias, w_out):
    B, L, _ = h.shape
    u = h @ w_in
    up = jnp.pad(u, ((0, 0), (1, 1), (0, 0)))
    u = up[:, :-2] * conv_w[0] + up[:, 1:-1] * conv_w[1] + up[:, 2:] * conv_w[2] + conv_b
    v, x1, x2 = jnp.split(u, 3, axis=-1)
    kf = implicit_filter_spectra(L, f_w1, f_b1, f_freq, f_w2, f_b2, f_w3)
    z = x1 * fft_long_conv(v, kf[:, 0], f_bias[0])
    z = x2 * fft_long_conv(z, kf[:, 1], f_bias[1])
    return z @ w_out


def moe(h, router_w, router_b, w1, w3, w2):
    B, L, D = h.shape
    x = h.reshape(B * L, D)
    scores = jax.nn.sigmoid((x @ router_w).astype(jnp.float32))
    sel = (scores + router_b.astype(jnp.float32)).reshape(-1, N_GROUPS, EXPERTS_PER_GROUP)
    group_score = lax.top_k(sel, 2)[0].sum(-1)
    g = jnp.argmax(group_score, axis=-1)
    in_group = jnp.take_along_axis(sel, g[:, None, None], axis=1)[:, 0]
    _, idx = lax.top_k(in_group, TOP_K)
    eid = g[:, None] * EXPERTS_PER_GROUP + idx
    w_sel = jnp.take_along_axis(scores, eid, axis=1)
    w_sel = w_sel / jnp.sum(w_sel, axis=-1, keepdims=True)
    gates = jnp.sum(jax.nn.one_hot(eid, N_EXPERTS, dtype=jnp.float32) * w_sel[..., None], axis=1)
    hdn = jax.nn.silu(jnp.einsum('td,edf->tef', x, w1)) * jnp.einsum('td,edf->tef', x, w3)
    y = jnp.einsum('tef,efd->td', hdn * gates.astype(x.dtype)[..., None], w2)
    return y.reshape(B, L, D)


def setup_inputs(seed: int = 0) -> dict:
    key = jax.random.key(seed)
    ks = iter(jax.random.split(key, 48))
    f32 = jnp.float32

    def nrm(shape, scale=1.0):
        return jax.random.normal(next(ks), shape, f32) * scale

    def gain(shape):
        return 1.0 + nrm(shape, 0.02)

    NA, NH, D = N_ATTN_LAYERS, N_HYENA_LAYERS, D_MODEL
    return {
        "x_prompt": nrm((BATCH, SEQ, D)),
        "x_sample": nrm((DEC_BATCH, DEC_SEQ, D)),
        "c": nrm((DEC_BATCH, D)),
        "cache_a_k": nrm((DEC_BATCH, NA, KV_A, PAST_LEN, HD_A)),
        "cache_a_v": nrm((DEC_BATCH, NA, KV_A, PAST_LEN, HD_A)),
        "cache_b_ckv": nrm((DEC_BATCH, NA, PAST_LEN, KV_LORA)),
        "cache_b_kpe": nrm((DEC_BATCH, NA, PAST_LEN, ROPE_B)),
        "c_ctx": nrm((D,)),
        "mod_w": nrm((DEPTH, D, 6 * D), 0.5 * D ** -0.5),
        "mod_b": nrm((DEPTH, 6 * D), 0.02),
        "norm_g": gain((DEPTH, 2, D)),
        "attn_w_in": nrm((NA, D, AB_IN), D ** -0.5),
        "a_q_norm": gain((NA, HD_A)),
        "a_k_norm": gain((NA, HD_A)),
        "a_sink": nrm((NA, H_A), 0.5),
        "b_q_a_norm": gain((NA, Q_LORA)),
        "b_w_uq": nrm((NA, Q_LORA, H_B * QK_B), Q_LORA ** -0.5),
        "b_kv_a_norm": gain((NA, KV_LORA)),
        "b_w_ukv": nrm((NA, KV_LORA, H_B * (NOPE_B + V_B)), KV_LORA ** -0.5),
        "b_q_norm": gain((NA, QK_B)),
        "b_k_norm": gain((NA, QK_B)),
        "attn_w_out": nrm((NA, AB_OUT, D), AB_OUT ** -0.5),
        "hy_w_in": nrm((NH, D, 3 * D), D ** -0.5),
        "hy_conv_w": nrm((NH, 3, 3 * D), 3 ** -0.5),
        "hy_conv_b": nrm((NH, 3 * D), 0.02),
        "hy_f_w1": nrm((NH, FILTER_EMB, FILTER_HID), FILTER_EMB ** -0.5),
        "hy_f_b1": nrm((NH, FILTER_HID), 0.5),
        "hy_f_freq": 1.0 + nrm((NH, 2, FILTER_HID), 0.1),
        "hy_f_w2": nrm((NH, FILTER_HID, FILTER_HID), FILTER_HID ** -0.5),
        "hy_f_b2": nrm((NH, FILTER_HID), 0.1),
        "hy_f_w3": nrm((NH, FILTER_HID, HYENA_ORDER * 2 * D), 0.1 * FILTER_HID ** -0.5),
        "hy_f_bias": nrm((NH, HYENA_ORDER, D), 0.5),
        "hy_w_out": nrm((NH, D, D), D ** -0.5),
        "router_w": nrm((D, N_EXPERTS), D ** -0.5),
        "router_b": nrm((N_EXPERTS,), 0.01),
        "moe_w1": nrm((DEPTH, N_EXPERTS, D, D_FF), D ** -0.5),
        "moe_w3": nrm((DEPTH, N_EXPERTS, D, D_FF), D ** -0.5),
        "moe_w2": nrm((DEPTH, N_EXPERTS, D_FF, D), D_FF ** -0.5),
    }


def reference(x_prompt, x_sample, c, cache_a_k, cache_a_v, cache_b_ckv, cache_b_kpe, c_ctx,
              mod_w, mod_b, norm_g, attn_w_in, a_q_norm, a_k_norm, a_sink, b_q_a_norm, b_w_uq,
              b_kv_a_norm, b_w_ukv, b_q_norm, b_k_norm, attn_w_out, hy_w_in, hy_conv_w, hy_conv_b,
              hy_f_w1, hy_f_b1, hy_f_freq, hy_f_w2, hy_f_b2, hy_f_w3, hy_f_bias, hy_w_out,
              router_w, router_b, moe_w1, moe_w3, moe_w2):
    y = x_prompt
    s = x_sample
    ks_, vs_, ckvs_, kpes_ = [], [], [], []
    for l in range(DEPTH):
        mp = adaln(c_ctx, mod_w[l], mod_b[l])
        ms = adaln(c, mod_w[l], mod_b[l])
        hp = rms_norm(y, norm_g[l, 0]) * (1.0 + mp[..., 1, :]) + mp[..., 0, :]
        hs = rms_norm(s, norm_g[l, 0]) * (1.0 + ms[..., 1, :]) + ms[..., 0, :]
        j = l // 2
        if l % 2 == 0:
            out_p, k_a, v_a, ckv, kpe = ab_context(
                hp, attn_w_in[j], a_q_norm[j], a_k_norm[j], a_sink[j], b_q_a_norm[j], b_w_uq[j],
                b_kv_a_norm[j], b_w_ukv[j], b_q_norm[j], b_k_norm[j], attn_w_out[j])
            ks_.append(k_a)
            vs_.append(v_a)
            ckvs_.append(ckv)
            kpes_.append(kpe)
            out_s = ab_latent(
                hs, cache_a_k[:, j], cache_a_v[:, j], cache_b_ckv[:, j], cache_b_kpe[:, j],
                attn_w_in[j], a_q_norm[j], a_k_norm[j], a_sink[j], b_q_a_norm[j], b_w_uq[j],
                b_kv_a_norm[j], b_w_ukv[j], b_q_norm[j], b_k_norm[j], attn_w_out[j])
        else:
            hy = (hy_w_in[j], hy_conv_w[j], hy_conv_b[j], hy_f_w1[j], hy_f_b1[j], hy_f_freq[j],
                  hy_f_w2[j], hy_f_b2[j], hy_f_w3[j], hy_f_bias[j], hy_w_out[j])
            out_p = hyena(hp, *hy)
            out_s = hyena(hs, *hy)
        y = y + mp[..., 2, :] * out_p
        s = s + ms[..., 2, :] * out_s
        hp = rms_norm(y, norm_g[l, 1]) * (1.0 + mp[..., 4, :]) + mp[..., 3, :]
        hs = rms_norm(s, norm_g[l, 1]) * (1.0 + ms[..., 4, :]) + ms[..., 3, :]
        y = y + mp[..., 5, :] * moe(hp, router_w, router_b, moe_w1[l], moe_w3[l], moe_w2[l])
        s = s + ms[..., 5, :] * moe(hs, router_w, router_b, moe_w1[l], moe_w3[l], moe_w2[l])
    new_a_k = jnp.stack(ks_, axis=1)
    new_a_v = jnp.stack(vs_, axis=1)
    new_b_ckv = jnp.stack(ckvs_, axis=1)
    new_b_kpe = jnp.stack(kpes_, axis=1)
    return (y, s, new_a_k, new_a_v, new_b_ckv, new_b_kpe)
```

```python
import functools
import math

import jax
import jax.numpy as jnp
import numpy as np
from jax import lax
from jax.experimental import pallas as pl
from jax.experimental.pallas import tpu as pltpu

F32 = jnp.float32
BF16 = jnp.bfloat16
I32 = jnp.int32

GRID_W = 64
H_A, KV_A, HD_A, WINDOW = 8, 2, 64, 128
H_B, NOPE_B, ROPE_B, V_B = 8, 64, 32, 64
QK_B = NOPE_B + ROPE_B
Q_LORA, KV_LORA = 384, 256
ROPE_THETA = 10000.0
HYENA_ORDER, N_BANDS, FILTER_HID = 2, 16, 64
FILTER_EMB = 1 + 2 * N_BANDS
FAST_DECAY_PCT, SLOW_DECAY_PCT, DECAY_TARGET = 0.3, 1.5, 1e-2
N_EXPERTS, N_GROUPS, TOP_K = 16, 4, 2
EPG = N_EXPERTS // N_GROUPS
EPS = 1e-6
NEG_INF = -1e30

LANES = 128
TM = 256
QB = 128
VMEM_LIMIT = 56 * 1024 * 1024

_O1 = H_A * HD_A
_O2 = _O1 + KV_A * HD_A
_O3 = _O2 + KV_A * HD_A
_O4 = _O3 + Q_LORA
_O5 = _O4 + KV_LORA
_O6 = _O5 + ROPE_B
_WIN_PAD = 1536

_PAIRS = ((0, 1), (0, 2), (0, 3), (1, 3), (1, 2), (3, 2))
N_CLASSES = N_GROUPS * len(_PAIRS)


def _cparams(sem, vmem=VMEM_LIMIT):
    return pltpu.CompilerParams(dimension_semantics=sem, vmem_limit_bytes=vmem)


def _dot(a, b):
    return jnp.dot(a, b, preferred_element_type=F32)


def _dot_t(a, b):
    return lax.dot_general(a, b, (((1,), (1,)), ((), ())), preferred_element_type=F32)


def _split(a):
    hi = a.astype(BF16)
    lo = (a - hi.astype(F32)).astype(BF16)
    return hi, lo


def _dot3(a, b):
    ah, al = _split(a)
    bh, bl = _split(b)
    return _dot(ah, bh) + _dot(al, bh) + _dot(ah, bl)


def _sigmoid(x):
    return 1.0 / (1.0 + jnp.exp(-x))


def _rms(x, g):
    ms = jnp.mean(x * x, axis=-1, keepdims=True)
    return x * lax.rsqrt(ms + EPS) * g


def _normmod(x, g, shift, scale):
    return _rms(x, g) * (1.0 + scale) + shift


def _rope(x, c, s):
    q = x.shape[-1] // 4
    partner = jnp.concatenate([x[:, q:2 * q], x[:, 0:q], x[:, 3 * q:4 * q], x[:, 2 * q:3 * q]], axis=1)
    return x * c + partner * s


def _rope_tail(x, c, s):
    return jnp.concatenate([x[:, :NOPE_B], _rope(x[:, NOPE_B:], c, s)], axis=1)


def _mod_kernel(ct_ref, w_ref, b_ref, o_ref):
    ct = ct_ref[...]
    s = ct * _sigmoid(ct)
    w = w_ref[0]
    for r in range(ct.shape[1]):
        o_ref[0, r:r + 1, :] = jnp.sum(w * s[:, r:r + 1], axis=0, keepdims=True) + b_ref[0]


def _modulation(cond, mod_w, mod_b):
    depth, d, n6 = mod_w.shape
    r = cond.shape[0]
    tn = 1536 if n6 % 1536 == 0 else n6
    out = pl.pallas_call(
        _mod_kernel,
        out_shape=jax.ShapeDtypeStruct((depth, r, n6), F32),
        grid=(depth, n6 // tn),
        in_specs=[pl.BlockSpec((d, r), lambda l, n: (0, 0)),
                  pl.BlockSpec((1, d, tn), lambda l, n: (l, 0, n)),
                  pl.BlockSpec((1, 1, tn), lambda l, n: (l, 0, n))],
        out_specs=pl.BlockSpec((1, r, tn), lambda l, n: (l, 0, n)),
        compiler_params=_cparams(("arbitrary", "arbitrary")),
        name="adaln_mod",
    )(cond.T, mod_w, mod_b.reshape(depth, 1, n6))
    return out.reshape(depth, r, 6, d)


def _resid_in(has_res, x_ref, m_ref, mp_ref):
    x = x_ref[...]
    if has_res:
        x = x + mp_ref[0, 0][5:6] * m_ref[...]
    return x


def _mla_kv_heads(ckv, kpe, wukv_ref, bkn, cb, sb, kb_ref, vb_ref):
    kv = _dot(ckv.astype(BF16), wukv_ref[...])
    for h in range(H_B):
        base = h * (NOPE_B + V_B)
        k = jnp.concatenate([kv[:, base:base + NOPE_B], kpe], axis=1)
        k = _rms(k, bkn)
        if cb is not None:
            k = _rope_tail(k, cb, sb)
        kb_ref[0, h] = k.astype(BF16)
        vb_ref[0, h] = kv[:, base + NOPE_B:base + NOPE_B + V_B].astype(BF16)


def _pre_attn_kernel(has_res, *refs):
    if has_res:
        x_ref, m_ref, mp_ref = refs[:3]
        refs = refs[3:]
    else:
        x_ref, m_ref, mp_ref = refs[0], None, None
        refs = refs[1:]
    (mod_ref, g_ref, win_ref, aqn_ref, akn_ref, bqan_ref, bwuq_ref, bkvan_ref, bwukv_ref, bqn_ref,
     bkn_ref, ca_ref, sa_ref, cb_ref, sb_ref) = refs[:15]
    outs = refs[15:]
    if has_res:
        x2_ref, outs = outs[0], outs[1:]
    qa_ref, ka_ref, va_ref, qb_ref, ckv_ref, kpe_ref, kb_ref, vb_ref = outs

    x = _resid_in(has_res, x_ref, m_ref, mp_ref)
    if has_res:
        x2_ref[...] = x
    m = mod_ref[0, 0]
    h = _normmod(x, g_ref[...], m[0:1], m[1:2])
    p = _dot(h.astype(BF16), win_ref[...])
    ca, sa = ca_ref[...], sa_ref[...]
    cb, sb = cb_ref[...], sb_ref[...]
    aqn, akn = aqn_ref[...], akn_ref[...]
    for hh in range(H_A):
        q = _rope(_rms(p[:, HD_A * hh:HD_A * (hh + 1)], aqn), ca, sa)
        qa_ref[0, hh] = (q * HD_A ** -0.5).astype(BF16)
    for hh in range(KV_A):
        k = _rope(_rms(p[:, _O1 + HD_A * hh:_O1 + HD_A * (hh + 1)], akn), ca, sa)
        ka_ref[0, hh] = k
        va_ref[0, hh] = p[:, _O2 + HD_A * hh:_O2 + HD_A * (hh + 1)]
    qlat = _rms(p[:, _O3:_O4], bqan_ref[...])
    qb_all = _dot(qlat.astype(BF16), bwuq_ref[...])
    bqn = bqn_ref[...]
    for hh in range(H_B):
        q = _rope_tail(_rms(qb_all[:, LANES * hh:LANES * hh + QK_B], bqn), cb, sb)
        qb_ref[0, hh] = (q * QK_B ** -0.5).astype(BF16)
    ckv = _rms(p[:, _O4:_O5], bkvan_ref[...])
    kpe = p[:, _O5:_O6]
    ckv_ref[...] = ckv
    kpe_ref[...] = kpe
    _mla_kv_heads(ckv, kpe, bwukv_ref, bkn_ref[...], cb, sb, kb_ref, vb_ref)


def _mla_cache_kernel(ckv_ref, kpe_ref, wukv_ref, bkn_ref, kb_ref, vb_ref):
    _mla_kv_heads(ckv_ref[...], kpe_ref[...], wukv_ref, bkn_ref[...], None, None, kb_ref, vb_ref)


def _softmax_pv(s, v, sink):
    mx = jnp.max(s, axis=-1, keepdims=True)
    if sink is not None:
        mx = jnp.maximum(mx, sink)
    p = jnp.exp(s - mx)
    den = jnp.sum(p, axis=-1, keepdims=True)
    if sink is not None:
        den = den + jnp.exp(sink - mx)
    return _dot(p.astype(BF16), v) / den


def _attn_kernel(latent, past, x_ref, mod_ref, wout_ref, qa_ref, ka_ref, va_ref, qb_ref, kb_ref, vb_ref,
                 sink_ref, *rest):
    if latent:
        _, o_ref, oscr = rest
    else:
        o_ref, oscr = rest
    nq = x_ref.shape[0]
    nk = ka_ref.shape[2]
    mask = None
    if latent:
        n_lat = nk - past
        qp = pl.program_id(1) * nq + lax.broadcasted_iota(I32, (nq, nk), 0)
        col = lax.broadcasted_iota(I32, (nq, nk), 1)
        mask = (col >= n_lat) | (jnp.abs(qp - col) <= WINDOW)
    g = H_A // KV_A
    for kh in range(KV_A):
        k = ka_ref[0, kh].astype(BF16)
        v = va_ref[0, kh].astype(BF16)
        for gi in range(g):
            hh = kh * g + gi
            s = _dot_t(qa_ref[0, hh], k)
            if mask is not None:
                s = jnp.where(mask, s, NEG_INF)
            oscr[:, HD_A * hh:HD_A * (hh + 1)] = _softmax_pv(s, v, sink_ref[hh])
    off = H_A * HD_A
    for hh in range(H_B):
        s = _dot_t(qb_ref[0, hh], kb_ref[0, hh])
        oscr[:, off + V_B * hh:off + V_B * (hh + 1)] = _softmax_pv(s, vb_ref[0, hh], None)
    y = _dot(oscr[...].astype(BF16), wout_ref[...])
    o_ref[...] = x_ref[...] + mod_ref[0, 0][2:3] * y


def _norm_matmul_kernel(has_res, *refs):
    if has_res:
        x_ref, m_ref, mp_ref, mod_ref, g_ref, w_ref, x2_ref, u_ref = refs
    else:
        x_ref, mod_ref, g_ref, w_ref, u_ref = refs
        m_ref = mp_ref = None
    x = _resid_in(has_res, x_ref, m_ref, mp_ref)
    if has_res:
        x2_ref[...] = x
    m = mod_ref[0, 0]
    h = _normmod(x, g_ref[...], m[0:1], m[1:2])
    u_ref[...] = _dot(h.astype(BF16), w_ref[...]).astype(u_ref.dtype)


def _proj_resid_kernel(x_ref, a_ref, mod_ref, w_ref, o_ref):
    o_ref[...] = x_ref[...] + mod_ref[0, 0][2:3] * _dot(a_ref[...], w_ref[...])


def _final_resid_kernel(x_ref, m_ref, mp_ref, o_ref):
    o_ref[...] = x_ref[...] + mp_ref[0, 0][5:6] * m_ref[...]


def _filter_kernel(feat_ref, w1_ref, b1_ref, fr_ref, w2_ref, b2_ref, w3f_ref, w3b_ref, dl_ref, cf_ref, sf_ref,
                   o_ref):
    L = feat_ref.shape[0]
    fr = fr_ref[...]
    hdn = jnp.sin(fr[0:1] * (_dot3(feat_ref[...], w1_ref[...]) + b1_ref[...]))
    hdn = jnp.sin(fr[1:2] * (_dot3(hdn, w2_ref[...]) + b2_ref[...]))
    row = lax.broadcasted_iota(I32, (L, 1), 0)
    t = row.astype(F32) / L
    dec = jnp.exp(-t * dl_ref[...])
    fwd = _dot3(hdn, w3f_ref[...]) * dec
    bwd = jnp.where(row == 0, 0.0, _dot3(hdn, w3b_ref[...]) * dec)
    e = (fwd + bwd).astype(BF16)
    o = (fwd - bwd).astype(BF16)
    pm = _dot(cf_ref[...], e)
    qm = _dot(sf_ref[...], o)
    nyq = _dot(sf_ref[0:8, :], e)[0:1]
    scale = jnp.where(row == 0, 0.5 / L, 1.0 / L)
    o_ref[0, 0] = pm * scale
    o_ref[0, 1] = jnp.where(row == 0, 0.0, qm) * scale
    o_ref[0, 2] = jnp.where(row == 0, nyq, pm) * scale


def _conv_kernel(aliased, uv_ref, u1_ref, u2_ref, wv_ref, w1_ref, w2_ref, bv_ref, b1_ref, b2_ref, spec_ref,
                 bias_ref, cf_ref, sf_ref, sft_ref, *rest):
    o_ref = rest[-1]
    L = uv_ref.shape[0]
    row = lax.broadcasted_iota(I32, (L, 1), 0)

    def sconv(u_ref, w_ref, b_ref):
        u = u_ref[...].astype(F32)
        up = jnp.where(row == 0, 0.0, pltpu.roll(u, 1, axis=0))
        un = jnp.where(row == L - 1, 0.0, pltpu.roll(u, L - 1, axis=0))
        w = w_ref[...]
        return up * w[0:1] + u * w[1:2] + un * w[2:3] + b_ref[...]

    def lconv(z, o):
        zb = z.astype(BF16)
        a = _dot(cf_ref[...], zb)
        b = _dot(sf_ref[...], zb)
        p, q, r = spec_ref[o, 0], spec_ref[o, 1], spec_ref[o, 2]
        yr = (a * p - b * q).astype(BF16)
        yi = (a * q + b * r).astype(BF16)
        y = _dot(cf_ref[...], yr) + _dot(sft_ref[...], yi)
        return y + z * bias_ref[o:o + 1, :]

    z = sconv(u1_ref, w1_ref, b1_ref) * lconv(sconv(uv_ref, wv_ref, bv_ref), 0)
    z = sconv(u2_ref, w2_ref, b2_ref) * lconv(z, 1)
    o_ref[...] = z.astype(o_ref.dtype)


def _router_kernel(x_ref, mod_ref, g_ref, rwh_ref, rwl_ref, rb_ref, o_ref):
    d = x_ref.shape[1]
    m = mod_ref[0, 0]
    h = _normmod(x_ref[...], g_ref[...], m[3:4], m[4:5])
    o_ref[:, 0:d] = h
    hh, hl = _split(h)
    logits = _dot(hh, rwh_ref[...]) + _dot(hl, rwh_ref[...]) + _dot(hh, rwl_ref[...])
    lane = lax.broadcasted_iota(I32, logits.shape, 1)
    valid = lane < N_EXPERTS
    score = _sigmoid(logits)
    sel = jnp.where(valid, score + rb_ref[...], -jnp.inf)

    def from_lane(x, k):
        return pltpu.roll(x, (LANES - k) % LANES, axis=1)

    pos = lane % EPG
    others = []
    for dlt in range(1, EPG):
        fwd_ok = pos + dlt < EPG
        val = jnp.where(fwd_ok, from_lane(sel, dlt), from_lane(sel, dlt - EPG))
        opos = jnp.where(fwd_ok, pos + dlt, pos + dlt - EPG)
        others.append((val, opos))
    pair = jnp.maximum(jnp.maximum(sel + others[0][0], sel + others[1][0]), sel + others[2][0])
    gs = pair
    for dlt in range(1, EPG):
        fwd_ok = pos + dlt < EPG
        gs = jnp.maximum(gs, jnp.where(fwd_ok, from_lane(pair, dlt), from_lane(pair, dlt - EPG)))
    grp = lane // EPG
    win = valid
    for dlt in range(1, N_GROUPS):
        k = dlt * EPG
        no_hi = grp + dlt >= N_GROUPS
        win = win & (no_hi | (gs >= from_lane(gs, k)))
        no_lo = grp - dlt < 0
        win = win & (no_lo | (gs > from_lane(gs, -k)))
    rank = jnp.zeros(sel.shape, F32)
    for val, opos in others:
        ahead = (val > sel) | ((val == sel) & (opos < pos))
        rank = rank + jnp.where(ahead, 1.0, 0.0)
    chosen = win & (rank < TOP_K)
    w = jnp.where(chosen, score, 0.0)
    gates = w / jnp.sum(w, axis=-1, keepdims=True)
    cf = jnp.where(chosen, 1.0, 0.0)
    gidx = jnp.sum(cf * grp.astype(F32), axis=-1, keepdims=True) * (1.0 / TOP_K)
    bits = jnp.sum(cf * jnp.where(pos == 0, 1.0, jnp.where(pos == 1, 2.0, jnp.where(pos == 2, 4.0, 8.0))),
                   axis=-1, keepdims=True)
    extra = gates + jnp.where(lane == N_EXPERTS, gidx, 0.0) + jnp.where(lane == N_EXPERTS + 1, bits, 0.0)
    o_ref[:, d:d + LANES] = extra


def _moe_kernel(ea_ref, eb_ref, ca_ref, cb_ref, nu_ref, tok_ref,
                h_hbm, w1a_ref, w3a_ref, w2a_ref, w1b_ref, w3b_ref, w2b_ref,
                o_hbm,
                xbuf, obuf, wa1, wa3, wa2, wb1, wb3, wb2, gsem, ssem):
    j = pl.program_id(0)
    nt = pl.num_programs(0)
    nu = nu_ref[0]
    tm = xbuf.shape[1]
    d = obuf.shape[2]

    def gather_copy(row, tok, slot):
        return pltpu.make_async_copy(h_hbm.at[pl.ds(tok, 1), :], xbuf.at[slot, pl.ds(row, 1), :], gsem.at[slot])

    def scatter_copy(row, tok, slot):
        return pltpu.make_async_copy(obuf.at[slot, pl.ds(row, 1), :], o_hbm.at[pl.ds(tok, 1), :], ssem.at[slot])

    def gather_all(slot):
        return pltpu.make_async_copy(h_hbm.at[pl.ds(0, tm), :], xbuf.at[slot], gsem.at[slot])

    def scatter_all(slot):
        return pltpu.make_async_copy(obuf.at[slot], o_hbm.at[pl.ds(0, tm), :], ssem.at[slot])

    n_tok = o_hbm.shape[0] - nt * tm

    def issue(copy_fn, scatter, tile, slot):
        base = tile * tm

        def body(i, c):
            for u in range(8):
                r = i * 8 + u
                tok = tok_ref[base + r]
                if scatter:
                    tok = jnp.where(tok < 0, n_tok + base + r, tok)
                else:
                    tok = jnp.maximum(tok, 0)
                copy_fn(r, tok, slot).start()
            return c

        lax.fori_loop(0, tm // 8, body, 0)

    @pl.when((j == 0) & (nu > 0))
    def _():
        issue(gather_copy, False, 0, 0)

    @pl.when(j < nu)
    def _():
        slot = j % 2
        gather_all(slot).wait()

        @pl.when(j + 1 < nu)
        def _():
            issue(gather_copy, False, j + 1, 1 - slot)

        @pl.when(ca_ref[j] == 1)
        def _():
            wa1[...] = w1a_ref[0, 0].astype(BF16)
            wa3[...] = w3a_ref[0, 0].astype(BF16)
            wa2[...] = w2a_ref[0, 0].astype(BF16)

        @pl.when(cb_ref[j] == 1)
        def _():
            wb1[...] = w1b_ref[0, 0].astype(BF16)
            wb3[...] = w3b_ref[0, 0].astype(BF16)
            wb2[...] = w2b_ref[0, 0].astype(BF16)

        xg = xbuf[slot]
        x = xg[:, 0:d].astype(BF16)
        e = xg[:, d:d + LANES]
        lane = lax.broadcasted_iota(I32, e.shape, 1)
        ga = jnp.sum(jnp.where(lane == ea_ref[j], e, 0.0), axis=-1, keepdims=True)
        gb = jnp.sum(jnp.where(lane == eb_ref[j], e, 0.0), axis=-1, keepdims=True)

        def hidden(w1, w3, gate):
            a = _dot(x, w1[...])
            return ((a * _sigmoid(a)) * _dot(x, w3[...]) * gate).astype(BF16)

        y = _dot(hidden(wa1, wa3, ga), wa2[...]) + _dot(hidden(wb1, wb3, gb), wb2[...])

        @pl.when(j >= 2)
        def _():
            scatter_all(slot).wait()

        obuf[slot] = y
        issue(scatter_copy, True, j, slot)

    @pl.when(j == nt - 1)
    def _():
        @pl.when(nu >= 1)
        def _():
            scatter_all((nu - 1) % 2).wait()

        @pl.when(nu >= 2)
        def _():
            scatter_all((nu - 2) % 2).wait()


def _rope_table(dec_seq, d):
    quarter = d // 4
    t = jnp.arange(dec_seq)
    row = (t // GRID_W).astype(F32)
    col = (t % GRID_W).astype(F32)
    freqs = ROPE_THETA ** (-jnp.arange(quarter, dtype=F32) / quarter)
    ar = row[:, None] * freqs[None, :]
    ac = col[:, None] * freqs[None, :]
    c = jnp.concatenate([jnp.cos(ar), jnp.cos(ar), jnp.cos(ac), jnp.cos(ac)], axis=1)
    s = jnp.concatenate([-jnp.sin(ar), jnp.sin(ar), -jnp.sin(ac), jnp.sin(ac)], axis=1)
    c = jnp.concatenate([jnp.ones((TM, d), F32), c], axis=0)
    s = jnp.concatenate([jnp.zeros((TM, d), F32), s], axis=0)
    return c, s


def _dft_mats(L):
    k = jnp.arange(L)
    kn = (k[:, None] * k[None, :]) % (2 * L)
    ang = kn.astype(F32) * (math.pi / L)
    cf = jnp.cos(ang)
    sf = -jnp.sin(ang)
    alt = jnp.where(k % 2 == 0, 1.0, -1.0).astype(F32)
    sf = sf.at[0, :].set(alt)
    return cf.astype(BF16), sf.astype(BF16), sf.T.astype(BF16)


def _filter_feats(L):
    t = jnp.arange(L, dtype=F32) / L
    bands = jnp.arange(1, N_BANDS + 1, dtype=F32)
    ang = 2.0 * math.pi * t[:, None] * bands[None, :]
    feats = jnp.concatenate([t[:, None], jnp.cos(ang), jnp.sin(ang)], axis=-1)
    return jnp.pad(feats, ((0, 0), (0, FILTER_HID - FILTER_EMB)))


class _Geom:
    def __init__(self, batch, seq, dec_batch, dec_seq, d):
        assert seq == TM and dec_seq % TM == 0 and d % LANES == 0
        self.batch, self.seq, self.dec_batch, self.dec_seq, self.d = batch, seq, dec_batch, dec_seq, d
        self.nc = batch * seq
        self.nl = dec_batch * dec_seq
        self.t = self.nc + self.nl
        self.nct = self.nc // TM
        self.tpl = dec_seq // TM
        self.nt = self.t // TM
        assert self.nc % dec_seq == 0

    def mod_row(self, i):
        return jnp.where(i < self.nct, 0, 1 + (i - self.nct) // self.tpl)

    def rope_blk(self, i):
        return jnp.where(i < self.nct, 0, 1 + (i - self.nct) % self.tpl)


def _tile_spec(d):
    return pl.BlockSpec((TM, d), lambda i: (i, 0))


def _mod_spec(geo, layer):
    return pl.BlockSpec((1, 1, 6, geo.d), lambda i: (layer, geo.mod_row(i), 0, 0))


def _const_spec(shape):
    nd = len(shape)
    return pl.BlockSpec(shape, lambda *a: (0,) * nd)


def _res_specs(geo, layer):
    return [_tile_spec(geo.d), _tile_spec(geo.d), _mod_spec(geo, layer - 1)]


def _attention_layer(geo, layer, x, moe_prev, mod, g, w, caches, tables):
    d, nt = geo.d, geo.nt
    has_res = moe_prev is not None
    (w_in, aqn, akn, sink, bqan, bwuq, bkvan, bwukv, bqn, bkn, w_out) = w
    ca, sa, cb, sb = tables
    cache_k, cache_v, cache_ckv, cache_kpe = caches
    past = cache_k.shape[2]

    w_in_p = jnp.pad(w_in, ((0, 0), (0, _WIN_PAD - w_in.shape[1]))).astype(BF16)
    bwuq_p = jnp.pad(bwuq.reshape(Q_LORA, H_B, QK_B), ((0, 0), (0, 0), (0, LANES - QK_B))).reshape(Q_LORA, H_B * LANES)
    bwuq_p = bwuq_p.astype(BF16)
    bwukv_b = bwukv.astype(BF16)
    w_out_b = w_out.astype(BF16)

    def hm(nh, dh, dt):
        return jax.ShapeDtypeStruct((nt, nh, TM, dh), dt), pl.BlockSpec((1, nh, TM, dh), lambda i: (i, 0, 0, 0))

    out_defs = [hm(H_A, HD_A, BF16), hm(KV_A, HD_A, F32), hm(KV_A, HD_A, F32), hm(H_B, QK_B, BF16),
                (jax.ShapeDtypeStruct((geo.t, KV_LORA), F32), _tile_spec(KV_LORA)),
                (jax.ShapeDtypeStruct((geo.t, ROPE_B), F32), _tile_spec(ROPE_B)),
                hm(H_B, QK_B, BF16), hm(H_B, V_B, BF16)]
    if has_res:
        out_defs = [(jax.ShapeDtypeStruct((geo.t, d), F32), _tile_spec(d))] + out_defs
    rope_a = pl.BlockSpec((TM, HD_A), lambda i: (geo.rope_blk(i), 0))
    rope_b = pl.BlockSpec((TM, ROPE_B), lambda i: (geo.rope_blk(i), 0))
    ins = ([x, moe_prev, mod] if has_res else [x]) + [
        mod, g, w_in_p, aqn[None], akn[None], bqan[None], bwuq_p, bkvan[None], bwukv_b, bqn[None], bkn[None],
        ca, sa, cb, sb]
    in_specs = (_res_specs(geo, layer) if has_res else [_tile_spec(d)]) + [
        _mod_spec(geo, layer), _const_spec((1, d)), _const_spec(w_in_p.shape), _const_spec((1, HD_A)),
        _const_spec((1, HD_A)), _const_spec((1, Q_LORA)), _const_spec(bwuq_p.shape), _const_spec((1, KV_LORA)),
        _const_spec(bwukv_b.shape), _const_spec((1, QK_B)), _const_spec((1, QK_B)), rope_a, rope_a, rope_b, rope_b]
    outs = pl.pallas_call(
        functools.partial(_pre_attn_kernel, has_res),
        out_shape=[o[0] for o in out_defs], grid=(nt,), in_specs=in_specs, out_specs=[o[1] for o in out_defs],
        compiler_params=_cparams(("arbitrary",)), name=f"pre_attn_{layer}",
    )(*ins)
    if has_res:
        x, outs = outs[0], outs[1:]
    qa, ka, va, qb, ckv, kpe, kb, vb = outs

    db = geo.dec_batch
    kbc, vbc = pl.pallas_call(
        _mla_cache_kernel,
        out_shape=[jax.ShapeDtypeStruct((db, H_B, past, QK_B), BF16), jax.ShapeDtypeStruct((db, H_B, past, V_B), BF16)],
        grid=(db,),
        in_specs=[pl.BlockSpec((None, past, KV_LORA), lambda b: (b, 0, 0)),
                  pl.BlockSpec((None, past, ROPE_B), lambda b: (b, 0, 0)),
                  _const_spec(bwukv_b.shape), _const_spec((1, QK_B))],
        out_specs=[pl.BlockSpec((1, H_B, past, QK_B), lambda b: (b, 0, 0, 0)),
                   pl.BlockSpec((1, H_B, past, V_B), lambda b: (b, 0, 0, 0))],
        compiler_params=_cparams(("arbitrary",)), name=f"mla_cache_{layer}",
    )(cache_ckv, cache_kpe, bwukv_b, bkn[None])

    nct = geo.nct

    def lat_seq(a, extra):
        nh, dh = a.shape[1], a.shape[3]
        la = a[nct:].reshape(db, geo.tpl, nh, TM, dh).transpose(0, 2, 1, 3, 4).reshape(db, nh, geo.dec_seq, dh)
        return jnp.concatenate([la, extra.astype(a.dtype)], axis=2)

    ka_l, va_l = lat_seq(ka, cache_k), lat_seq(va, cache_v)
    kb_l, vb_l = lat_seq(kb, kbc), lat_seq(vb, vbc)
    nk = geo.dec_seq + past

    sink_spec = pl.BlockSpec(memory_space=pltpu.SMEM)
    scratch = [pltpu.VMEM((TM, H_A * HD_A + H_B * V_B), F32)]
    y = pl.pallas_call(
        functools.partial(_attn_kernel, False, 0),
        out_shape=jax.ShapeDtypeStruct((geo.t, d), F32), grid=(nct,),
        in_specs=[_tile_spec(d), _mod_spec(geo, layer), _const_spec(w_out_b.shape),
                  pl.BlockSpec((1, H_A, TM, HD_A), lambda i: (i, 0, 0, 0)),
                  pl.BlockSpec((1, KV_A, TM, HD_A), lambda i: (i, 0, 0, 0)),
                  pl.BlockSpec((1, KV_A, TM, HD_A), lambda i: (i, 0, 0, 0)),
                  pl.BlockSpec((1, H_B, TM, QK_B), lambda i: (i, 0, 0, 0)),
                  pl.BlockSpec((1, H_B, TM, QK_B), lambda i: (i, 0, 0, 0)),
                  pl.BlockSpec((1, H_B, TM, V_B), lambda i: (i, 0, 0, 0)),
                  sink_spec],
        out_specs=_tile_spec(d), scratch_shapes=scratch,
        compiler_params=_cparams(("arbitrary",)), name=f"attn_ctx_{layer}",
    )(x, mod, w_out_b, qa, ka, va, qb, kb, vb, sink)

    nqb = geo.dec_seq // QB
    per = TM // QB

    def xrow(b, n):
        return (geo.nc + b * geo.dec_seq) // QB + n

    def qspec(nh, dh):
        return pl.BlockSpec((1, nh, QB, dh), lambda b, n: (nct + b * geo.tpl + n // per, 0, n % per, 0))

    def kspec(nh, dh):
        return pl.BlockSpec((1, nh, nk, dh), lambda b, n: (b, 0, 0, 0))

    y = pl.pallas_call(
        functools.partial(_attn_kernel, True, past),
        out_shape=jax.ShapeDtypeStruct((geo.t, d), F32), grid=(db, nqb),
        in_specs=[pl.BlockSpec((QB, d), lambda b, n: (xrow(b, n), 0)),
                  pl.BlockSpec((1, 1, 6, d), lambda b, n: (layer, 1 + b, 0, 0)),
                  pl.BlockSpec(w_out_b.shape, lambda b, n: (0, 0)),
                  qspec(H_A, HD_A), kspec(KV_A, HD_A), kspec(KV_A, HD_A),
                  qspec(H_B, QK_B), kspec(H_B, QK_B), kspec(H_B, V_B),
                  sink_spec, pl.BlockSpec(memory_space=pl.ANY)],
        out_specs=pl.BlockSpec((QB, d), lambda b, n: (xrow(b, n), 0)),
        scratch_shapes=[pltpu.VMEM((QB, H_A * HD_A + H_B * V_B), F32)],
        input_output_aliases={10: 0},
        compiler_params=_cparams(("arbitrary", "arbitrary")), name=f"attn_lat_{layer}",
    )(x, mod, w_out_b, qa, ka_l, va_l, qb, kb_l, vb_l, sink, y)

    b, s = geo.batch, geo.seq
    new = (ka[:nct].reshape(b, KV_A, s, HD_A), va[:nct].reshape(b, KV_A, s, HD_A),
           ckv[:geo.nc].reshape(b, s, KV_LORA), kpe[:geo.nc].reshape(b, s, ROPE_B))
    return y, new


def _filter_spectra(L, d, hy, mats):
    (f_w1, f_b1, f_freq, f_w2, f_b2, f_w3, deltas) = hy
    cf, sf, _ = mats
    tn = 256
    nd = d // tn
    w1p = jnp.pad(f_w1, ((0, FILTER_HID - FILTER_EMB), (0, 0)))
    return pl.pallas_call(
        _filter_kernel,
        out_shape=jax.ShapeDtypeStruct((HYENA_ORDER, 3, L, d), F32),
        grid=(HYENA_ORDER, nd),
        in_specs=[_const_spec((L, FILTER_HID)), _const_spec((FILTER_HID, FILTER_HID)), _const_spec((1, FILTER_HID)),
                  _const_spec((2, FILTER_HID)), _const_spec((FILTER_HID, FILTER_HID)), _const_spec((1, FILTER_HID)),
                  pl.BlockSpec((FILTER_HID, tn), lambda o, n: (0, o * 2 * nd + n)),
                  pl.BlockSpec((FILTER_HID, tn), lambda o, n: (0, o * 2 * nd + nd + n)),
                  pl.BlockSpec((1, tn), lambda o, n: (0, n)),
                  _const_spec((L, L)), _const_spec((L, L))],
        out_specs=pl.BlockSpec((1, 3, L, tn), lambda o, n: (o, 0, 0, n)),
        compiler_params=_cparams(("arbitrary", "arbitrary")), name=f"hy_filter_{L}",
    )(_filter_feats(L), w1p, f_b1[None], f_freq, f_w2, f_b2[None], f_w3, f_w3, deltas, cf, sf)


def _long_conv(geo, L, nseq, row0, tn, u, z_prev, conv_w, conv_b, spec, f_bias, mats, name):
    d = geo.d
    nd = d // tn
    cf, sf, sft = mats
    rb0 = row0 // L
    aliased = z_prev is not None

    def ucol(k):
        return pl.BlockSpec((L, tn), lambda n, b: (rb0 + b, k * nd + n))

    def wcol(k, rows):
        return pl.BlockSpec((rows, tn), lambda n, b: (0, k * nd + n))

    in_specs = [ucol(0), ucol(1), ucol(2), wcol(0, 3), wcol(1, 3), wcol(2, 3), wcol(0, 1), wcol(1, 1), wcol(2, 1),
                pl.BlockSpec((HYENA_ORDER, 3, L, tn), lambda n, b: (0, 0, 0, n)),
                pl.BlockSpec((HYENA_ORDER, tn), lambda n, b: (0, n)),
                _const_spec((L, L)), _const_spec((L, L)), _const_spec((L, L))]
    args = [u, u, u, conv_w, conv_w, conv_w, conv_b, conv_b, conv_b, spec, f_bias, cf, sf, sft]
    kw = {}
    if aliased:
        in_specs.append(pl.BlockSpec(memory_space=pl.ANY))
        args.append(z_prev)
        kw["input_output_aliases"] = {len(args) - 1: 0}
    return pl.pallas_call(
        functools.partial(_conv_kernel, aliased),
        out_shape=jax.ShapeDtypeStruct((geo.t, d), BF16), grid=(nd, nseq),
        in_specs=in_specs, out_specs=pl.BlockSpec((L, tn), lambda n, b: (rb0 + b, n)),
        compiler_params=_cparams(("arbitrary", "arbitrary")), name=name, **kw,
    )(*args)


def _hyena_layer(geo, layer, x, moe_prev, mod, g, w, consts):
    d, nt = geo.d, geo.nt
    has_res = moe_prev is not None
    (w_in, conv_w, conv_b, f_w1, f_b1, f_freq, f_w2, f_b2, f_w3, f_bias, w_out) = w
    mats_c, mats_l, deltas = consts
    w_in_b = w_in.astype(BF16)
    ins = ([x, moe_prev, mod] if has_res else [x]) + [mod, g, w_in_b]
    in_specs = (_res_specs(geo, layer) if has_res else [_tile_spec(d)]) + [
        _mod_spec(geo, layer), _const_spec((1, d)), _const_spec(w_in_b.shape)]
    out_shape = [jax.ShapeDtypeStruct((geo.t, 3 * d), BF16)]
    out_specs = [_tile_spec(3 * d)]
    if has_res:
        out_shape = [jax.ShapeDtypeStruct((geo.t, d), F32)] + out_shape
        out_specs = [_tile_spec(d)] + out_specs
    outs = pl.pallas_call(
        functools.partial(_norm_matmul_kernel, has_res),
        out_shape=out_shape, grid=(nt,), in_specs=in_specs, out_specs=out_specs,
        compiler_params=_cparams(("arbitrary",)), name=f"hy_in_{layer}",
    )(*ins)
    if has_res:
        x, u = outs
    else:
        (u,) = outs

    hy = (f_w1, f_b1, f_freq, f_w2, f_b2, f_w3, deltas)
    spec_c = _filter_spectra(geo.seq, d, hy, mats_c)
    spec_l = _filter_spectra(geo.dec_seq, d, hy, mats_l)
    cb2 = conv_b[None]
    z = _long_conv(geo, geo.seq, geo.batch, 0, d, u, None, conv_w, cb2, spec_c, f_bias, mats_c, f"hy_conv_ctx_{layer}")
    z = _long_conv(geo, geo.dec_seq, geo.dec_batch, geo.nc, 256, u, z, conv_w, cb2, spec_l, f_bias, mats_l,
                   f"hy_conv_lat_{layer}")
    w_out_b = w_out.astype(BF16)
    y = pl.pallas_call(
        _proj_resid_kernel,
        out_shape=jax.ShapeDtypeStruct((geo.t, d), F32), grid=(nt,),
        in_specs=[_tile_spec(d), _tile_spec(d), _mod_spec(geo, layer), _const_spec(w_out_b.shape)],
        out_specs=_tile_spec(d),
        compiler_params=_cparams(("arbitrary",)), name=f"hy_out_{layer}",
    )(x, z, mod, w_out_b)
    return y


def _moe_layer(geo, layer, x, mod, g, router_w, router_b, w1, w3, w2):
    d, nt, t = geo.d, geo.nt, geo.t
    f = w1.shape[-1]
    rw = jnp.pad(router_w, ((0, 0), (0, LANES - N_EXPERTS)))
    rwh = rw.astype(BF16)
    rwl = (rw - rwh.astype(F32)).astype(BF16)
    rb = jnp.pad(router_b, (0, LANES - N_EXPERTS))[None]
    hext = pl.pallas_call(
        _router_kernel,
        out_shape=jax.ShapeDtypeStruct((t, d + LANES), F32), grid=(nt,),
        in_specs=[_tile_spec(d), _mod_spec(geo, layer), _const_spec((1, d)), _const_spec(rwh.shape),
                  _const_spec(rwl.shape), _const_spec((1, LANES))],
        out_specs=_tile_spec(d + LANES),
        compiler_params=_cparams(("arbitrary",)), name=f"router_{layer}",
    )(x, mod, g, rwh, rwl, rb)

    info = hext[:, d + N_EXPERTS:d + N_EXPERTS + 2]
    gidx = info[:, 0].astype(I32)
    bits = info[:, 1].astype(I32)
    lut = np.zeros((16,), np.int32)
    for ci, (a, b) in enumerate(_PAIRS):
        lut[(1 << a) | (1 << b)] = ci
    cls = gidx * len(_PAIRS) + jnp.asarray(lut)[bits]
    onehot = (cls[:, None] == jnp.arange(N_CLASSES)[None, :]).astype(I32)
    counts = jnp.sum(onehot, axis=0)
    rank = jnp.sum((jnp.cumsum(onehot, axis=0) - onehot) * onehot, axis=1)
    padded = ((counts + TM - 1) // TM) * TM
    ends = jnp.cumsum(padded)
    starts = ends - padded
    pos = starts[cls] + rank
    ntm = nt + N_CLASSES
    p_rows = ntm * TM
    tok = jnp.arange(t, dtype=I32)
    row_tok = jnp.full((p_rows,), -1, I32).at[pos].set(tok)
    n_used = (ends[-1] // TM).astype(I32)
    tile_cls = jnp.searchsorted(ends, jnp.arange(ntm, dtype=I32) * TM, side="right").astype(I32)
    last_cls = jnp.searchsorted(ends, jnp.maximum(n_used - 1, 0) * TM, side="right").astype(I32)
    tile_cls = jnp.minimum(jnp.where(jnp.arange(ntm) < n_used, tile_cls, last_cls), N_CLASSES - 1)
    pa = jnp.asarray(np.array([p[0] for p in _PAIRS], np.int32))
    pb = jnp.asarray(np.array([p[1] for p in _PAIRS], np.int32))
    grp = tile_cls // len(_PAIRS)
    ea = grp * EPG + pa[tile_cls % len(_PAIRS)]
    eb = grp * EPG + pb[tile_cls % len(_PAIRS)]
    first = jnp.arange(ntm) == 0
    cha = (first | (ea != jnp.roll(ea, 1))).astype(I32)
    chb = (first | (eb != jnp.roll(eb, 1))).astype(I32)

    def wspec(which):
        if which == 0:
            return pl.BlockSpec((1, 1) + w1.shape[2:], lambda j, ea_r, eb_r, *_: (layer, ea_r[j], 0, 0))
        return pl.BlockSpec((1, 1) + w1.shape[2:], lambda j, ea_r, eb_r, *_: (layer, eb_r[j], 0, 0))

    def w2spec(which):
        if which == 0:
            return pl.BlockSpec((1, 1) + w2.shape[2:], lambda j, ea_r, eb_r, *_: (layer, ea_r[j], 0, 0))
        return pl.BlockSpec((1, 1) + w2.shape[2:], lambda j, ea_r, eb_r, *_: (layer, eb_r[j], 0, 0))

    out = pl.pallas_call(
        _moe_kernel,
        out_shape=jax.ShapeDtypeStruct((t + p_rows, d), F32),
        grid_spec=pltpu.PrefetchScalarGridSpec(
            num_scalar_prefetch=6, grid=(ntm,),
            in_specs=[pl.BlockSpec(memory_space=pl.ANY), wspec(0), wspec(0), w2spec(0), wspec(1), wspec(1), w2spec(1)],
            out_specs=pl.BlockSpec(memory_space=pl.ANY),
            scratch_shapes=[pltpu.VMEM((2, TM, d + LANES), F32), pltpu.VMEM((2, TM, d), F32),
                            pltpu.VMEM((d, f), BF16), pltpu.VMEM((d, f), BF16), pltpu.VMEM((f, d), BF16),
                            pltpu.VMEM((d, f), BF16), pltpu.VMEM((d, f), BF16), pltpu.VMEM((f, d), BF16),
                            pltpu.SemaphoreType.DMA((2,)), pltpu.SemaphoreType.DMA((2,))]),
        compiler_params=_cparams(("arbitrary",)), name=f"moe_experts_{layer}",
    )(ea, eb, cha, chb, n_used[None], row_tok, hext, w1, w3, w2, w1, w3, w2)
    return out


def kernel(x_prompt, x_sample, c, cache_a_k, cache_a_v, cache_b_ckv, cache_b_kpe, c_ctx, mod_w, mod_b, norm_g, attn_w_in, a_q_norm, a_k_norm, a_sink, b_q_a_norm, b_w_uq, b_kv_a_norm, b_w_ukv, b_q_norm, b_k_norm, attn_w_out, hy_w_in, hy_conv_w, hy_conv_b, hy_f_w1, hy_f_b1, hy_f_freq, hy_f_w2, hy_f_b2, hy_f_w3, hy_f_bias, hy_w_out, router_w, router_b, moe_w1, moe_w3, moe_w2):
    batch, seq, d = x_prompt.shape
    dec_batch, dec_seq, _ = x_sample.shape
    depth = mod_w.shape[0]
    geo = _Geom(batch, seq, dec_batch, dec_seq, d)

    x = jnp.concatenate([x_prompt.reshape(geo.nc, d), x_sample.reshape(geo.nl, d)], axis=0)
    mod = _modulation(jnp.concatenate([c_ctx[None], c], axis=0), mod_w, mod_b)

    ca, sa = _rope_table(dec_seq, HD_A)
    cb, sb = _rope_table(dec_seq, ROPE_B)
    max_decay = math.log(DECAY_TARGET) / FAST_DECAY_PCT
    min_decay = math.log(DECAY_TARGET) / SLOW_DECAY_PCT
    deltas = jnp.abs(jnp.linspace(min_decay, max_decay, d, dtype=F32))[None]
    hy_consts = (_dft_mats(seq), _dft_mats(dec_seq), deltas)

    moe_prev = None
    news = []
    for l in range(depth):
        j = l // 2
        g1 = norm_g[l, 0][None]
        g2 = norm_g[l, 1][None]
        if l % 2 == 0:
            w = (attn_w_in[j], a_q_norm[j], a_k_norm[j], a_sink[j], b_q_a_norm[j], b_w_uq[j], b_kv_a_norm[j],
                 b_w_ukv[j], b_q_norm[j], b_k_norm[j], attn_w_out[j])
            caches = (cache_a_k[:, j], cache_a_v[:, j], cache_b_ckv[:, j], cache_b_kpe[:, j])
            x, new = _attention_layer(geo, l, x, moe_prev, mod, g1, w, caches, (ca, sa, cb, sb))
            news.append(new)
        else:
            w = (hy_w_in[j], hy_conv_w[j], hy_conv_b[j], hy_f_w1[j], hy_f_b1[j], hy_f_freq[j], hy_f_w2[j],
                 hy_f_b2[j], hy_f_w3[j], hy_f_bias[j], hy_w_out[j])
            x = _hyena_layer(geo, l, x, moe_prev, mod, g1, w, hy_consts)
        moe_prev = _moe_layer(geo, l, x, mod, g2, router_w, router_b, moe_w1, moe_w3, moe_w2)

    y = pl.pallas_call(
        _final_resid_kernel,
        out_shape=jax.ShapeDtypeStruct((geo.t, d), F32), grid=(geo.nt,),
        in_specs=[_tile_spec(d), _tile_spec(d), _mod_spec(geo, depth - 1)],
        out_specs=_tile_spec(d),
        compiler_params=_cparams(("arbitrary",)), name="final_resid",
    )(x, moe_prev, mod)

    y_prompt = y[:geo.nc].reshape(batch, seq, d)
    y_sample = y[geo.nc:].reshape(dec_batch, dec_seq, d)
    new_a_k = jnp.stack([n[0] for n in news], axis=1)
    new_a_v = jnp.stack([n[1] for n in news], axis=1)
    new_b_ckv = jnp.stack([n[2] for n in news], axis=1)
    new_b_kpe = jnp.stack([n[3] for n in news], axis=1)
    return (y_prompt, y_sample, new_a_k, new_a_v, new_b_ckv, new_b_kpe)
```
